```python
import jax, jax.numpy as jnp
from jax import lax
import numpy as np

D_MODEL = 1024
BATCH = 8
SEQ = 2048
DEPTH = 2

D_MIX = D_MODEL
D_POOL = D_MIX // 2
D_SGU = D_MIX - D_POOL
POOL_WINDOWS = (2, 4, 8, 16)
N_POOL_GROUPS = len(POOL_WINDOWS)
POOL_GROUP = D_POOL // N_POOL_GROUPS
N_SGU_HEADS = 4
SGU_HEAD = D_SGU // N_SGU_HEADS
CHUNK = 128
SPLITS = (D_POOL, 2 * D_POOL, 2 * D_POOL + D_SGU, 2 * D_POOL + 2 * D_SGU)
D_IN_PROJ = 2 * D_POOL + 3 * D_SGU
DEEPNORM_ALPHA = (2.0 * DEPTH) ** 0.25
DEEPNORM_BETA = (8.0 * DEPTH) ** -0.25
LN_EPS = 1e-5

kernel_name = "hybrid_pool_sgu_deepnorm_adaln"


def _layer_norm(x):
    xf = x.astype(jnp.float32)
    mu = jnp.mean(xf, axis=-1, keepdims=True)
    var = jnp.mean(jnp.square(xf - mu), axis=-1, keepdims=True)
    return ((xf - mu) * lax.rsqrt(var + LN_EPS)).astype(x.dtype)


def _pool_mixer(xa, w_pool, pool_scale):
    B, S, _ = xa.shape
    xg = xa.reshape(B, S, N_POOL_GROUPS, POOL_GROUP)
    cs = jnp.cumsum(xg.astype(jnp.float32), axis=1)
    cs = jnp.pad(cs, ((0, 0), (1, 0), (0, 0), (0, 0)))
    t = jnp.arange(S)
    pooled = []
    for g, w in enumerate(POOL_WINDOWS):
        lo = jnp.maximum(t + 1 - w, 0)
        cs_g = cs[:, :, g, :]
        win_sum = cs_g[:, 1:] - cs_g[:, lo]
        count = (t + 1 - lo).astype(jnp.float32)
        pooled.append(win_sum / count[None, :, None])
    pooled = jnp.stack(pooled, axis=2).astype(xa.dtype) - xg
    y = jnp.einsum('bsgc,gcd->bsgd', pooled, w_pool)
    return y.reshape(B, S, D_POOL) * pool_scale


def _sgu_mixer(u, v, ln_g, ln_b, w_s, b_s):
    B, S, _ = u.shape
    n_chunks = S // CHUNK
    v = v.reshape(B, n_chunks, CHUNK, N_SGU_HEADS, SGU_HEAD)
    v = _layer_norm(v) * ln_g + ln_b
    causal = jnp.tril(jnp.ones((CHUNK, CHUNK), dtype=bool))
    w = jnp.where(causal[None], w_s, 0)
    mixed = jnp.einsum('hts,bnshd->bnthd', w, v) + jnp.transpose(b_s)[:, :, None]
    return u * mixed.reshape(B, S, D_SGU)


def _hybrid_layer(x, c, w_ada, b_ada, w_in, w_pool, pool_scale,
                  sgu_ln_g, sgu_ln_b, w_sgu, b_sgu, w_out, ln_g, ln_b):
    mod = jax.nn.silu(c) @ w_ada + b_ada
    shift, scale, gate = jnp.split(mod, 3, axis=-1)
    h = _layer_norm(x) * (1 + scale[:, None]) + shift[:, None]
    proj = h @ w_in
    xa, ga, u, v, gb = jnp.split(proj, SPLITS, axis=-1)
    ya = _pool_mixer(xa, w_pool, pool_scale) * jax.nn.silu(ga)
    yb = _sgu_mixer(jax.nn.gelu(u, approximate=False), jax.nn.gelu(v, approximate=False),
                    sgu_ln_g, sgu_ln_b, w_sgu, b_sgu) * jax.nn.silu(gb)
    y = jnp.concatenate([ya, yb], axis=-1) @ w_out
    z = DEEPNORM_ALPHA * x + gate[:, None] * y
    return _layer_norm(z) * ln_g + ln_b


def setup_inputs(seed: int = 0) -> dict:
    key = jax.random.key(seed)
    ks = jax.random.split(key, 16)
    f32 = jnp.float32
    x = jax.random.normal(ks[0], (BATCH, SEQ, D_MODEL), f32)
    c = jax.random.normal(ks[1], (BATCH, D_MODEL), f32)
    w_ada = jax.random.normal(ks[2], (DEPTH, D_MODEL, 3 * D_MODEL), f32) * (0.5 * D_MODEL ** -0.5)
    b_ada = jax.random.normal(ks[3], (DEPTH, 3 * D_MODEL), f32) * 0.02
    w_in = jax.random.normal(ks[4], (DEPTH, D_MODEL, D_IN_PROJ), f32) * D_MODEL ** -0.5
    w_pool = jax.random.normal(ks[5], (DEPTH, N_POOL_GROUPS, POOL_GROUP, POOL_GROUP), f32) * POOL_GROUP ** -0.5
    pool_scale = 1.0 + 0.1 * jax.random.normal(ks[6], (DEPTH, D_POOL), f32)
    sgu_ln_g = 1.0 + 0.1 * jax.random.normal(ks[7], (DEPTH, N_SGU_HEADS, SGU_HEAD), f32)
    sgu_ln_b = 0.02 * jax.random.normal(ks[8], (DEPTH, N_SGU_HEADS, SGU_HEAD), f32)
    w_sgu = jax.random.normal(ks[9], (DEPTH, N_SGU_HEADS, CHUNK, CHUNK), f32) * CHUNK ** -0.5
    b_sgu = 1.0 + 0.1 * jax.random.normal(ks[10], (DEPTH, N_SGU_HEADS, CHUNK), f32)
    w_out = jax.random.normal(ks[11], (DEPTH, D_MIX, D_MODEL), f32) * (D_MIX ** -0.5 * DEEPNORM_BETA)
    ln_g = 1.0 + 0.1 * jax.random.normal(ks[12], (DEPTH, D_MODEL), f32)
    ln_b = 0.02 * jax.random.normal(ks[13], (DEPTH, D_MODEL), f32)
    return {"x": x, "c": c, "w_ada": w_ada, "b_ada": b_ada, "w_in": w_in,
            "w_pool": w_pool, "pool_scale": pool_scale, "sgu_ln_g": sgu_ln_g,
            "sgu_ln_b": sgu_ln_b, "w_sgu": w_sgu, "b_sgu": b_sgu, "w_out": w_out,
            "ln_g": ln_g, "ln_b": ln_b}


def reference(x, c, w_ada, b_ada, w_in, w_pool, pool_scale, sgu_ln_g, sgu_ln_b,
              w_sgu, b_sgu, w_out, ln_g, ln_b):
    for l in range(DEPTH):
        x = _hybrid_layer(x, c, w_ada[l], b_ada[l], w_in[l], w_pool[l], pool_scale[l],
                          sgu_ln_g[l], sgu_ln_b[l], w_sgu[l], b_sgu[l], w_out[l],
                          ln_g[l], ln_b[l])
    return x
```

```python
import functools
import math

import jax
import jax.numpy as jnp
from jax import lax
from jax.experimental import pallas as pl
from jax.experimental.pallas import tpu as pltpu

POOL_WINDOWS = (2, 4, 8, 16)
N_SGU_HEADS = 4
CHUNK = 128
LANES = 128
HALO = 16
LN_EPS = 1e-5
ROW_BLOCK = 256
VMEM_LIMIT_BYTES = 48 * 1024 * 1024

_BF16 = jnp.bfloat16
_F32 = jnp.float32


def _layer_norm(x):
    mu = jnp.mean(x, axis=-1, keepdims=True)
    xc = x - mu
    var = jnp.mean(xc * xc, axis=-1, keepdims=True)
    return xc * lax.rsqrt(var + LN_EPS)


def _gelu(x):
    return 0.5 * x * (1.0 + lax.erf(x * (1.0 / math.sqrt(2.0))))


def _layer_block_index(layer, *tail):
    return lambda b, j: (layer,) + tail


def _mod_block_index(layer, batch, b, j):
    return (layer * batch + b, 0, 0)


def _row_block_index(b, j):
    return (b, j, 0)


def _mod_kernel(c_ref, w_ref, b_ref, o_ref):
    sc = jax.nn.silu(c_ref[...]).astype(_BF16)
    o_ref[0] = jnp.dot(sc, w_ref[0].astype(_BF16), preferred_element_type=_F32) + b_ref[0]


def _layer_kernel(x_ref, mod_ref, w_in_ref, w_pool_ref, pool_scale_ref, sgu_g_ref, sgu_b_ref,
                  w_sgu_ref, sgu_bias_ref, w_out_ref, ln_g_ref, ln_b_ref, o_ref, ext_ref,
                  *, alpha, d_pool, d_sgu):
    j = pl.program_id(1)
    rb = x_ref.shape[1]
    x = x_ref[0]
    mod = mod_ref[0]
    shift, scale, gate = mod[0:1], mod[1:2], mod[2:3]

    hb = (_layer_norm(x) * (1.0 + scale) + shift).astype(_BF16)

    def proj(lo, width):
        return jnp.dot(hb, w_in_ref[0, :, lo:lo + width], preferred_element_type=_F32)

    xa = proj(0, d_pool)
    ga = proj(d_pool, d_pool)

    @pl.when(j == 0)
    def _():
        ext_ref[0:HALO, :] = jnp.zeros((HALO, d_pool), _F32)

    ext_ref[HALO:HALO + rb, :] = xa
    t = j * rb + lax.broadcasted_iota(jnp.int32, (rb, 1), 0)
    ya_parts = []
    for g, w in enumerate(POOL_WINDOWS):
        lanes = slice(g * LANES, (g + 1) * LANES)
        acc = ext_ref[:, lanes]
        span = 1
        while span < w:
            acc = acc + pltpu.roll(acc, span, axis=0)
            span *= 2
        win_sum = acc[HALO:, :]
        inv_count = 1.0 / jnp.minimum(t + 1, w).astype(_F32)
        pooled = win_sum * inv_count - xa[:, lanes]
        yg = jnp.dot(pooled.astype(_BF16), w_pool_ref[0, g], preferred_element_type=_F32)
        ya_parts.append(yg * pool_scale_ref[0, :, lanes] * jax.nn.silu(ga[:, lanes]))
    ext_ref[0:HALO, :] = xa[rb - HALO:, :]

    u = _gelu(proj(2 * d_pool, d_sgu))
    v = _gelu(proj(2 * d_pool + d_sgu, d_sgu))
    gb = proj(2 * d_pool + 2 * d_sgu, d_sgu)
    n_chunks = rb // CHUNK
    yb_parts = []
    for h in range(N_SGU_HEADS):
        lanes = slice(h * LANES, (h + 1) * LANES)
        vn = (_layer_norm(v[:, lanes]) * sgu_g_ref[0, :, lanes] + sgu_b_ref[0, :, lanes]).astype(_BF16)
        rhs = jnp.concatenate([vn[c * CHUNK:(c + 1) * CHUNK, :] for c in range(n_chunks)], axis=1)
        mixed = jnp.dot(w_sgu_ref[0, h], rhs, preferred_element_type=_F32)
        mixed = jnp.concatenate(
            [mixed[:, c * LANES:(c + 1) * LANES] + sgu_bias_ref[0, h] for c in range(n_chunks)], axis=0)
        yb_parts.append(u[:, lanes] * mixed * jax.nn.silu(gb[:, lanes]))

    ycat = jnp.concatenate(ya_parts + yb_parts, axis=1).astype(_BF16)
    y = jnp.dot(ycat, w_out_ref[0], preferred_element_type=_F32)

    z = alpha * x + gate * y
    o_ref[0] = _layer_norm(z) * ln_g_ref[0] + ln_b_ref[0]


def kernel(x, c, w_ada, b_ada, w_in, w_pool, pool_scale, sgu_ln_g, sgu_ln_b, w_sgu, b_sgu, w_out, ln_g, ln_b):
    batch, seq, d_model = x.shape
    depth = w_ada.shape[0]
    d_pool = w_pool.shape[1] * w_pool.shape[2]
    d_sgu = w_sgu.shape[1] * sgu_ln_g.shape[2]
    d_in_proj = w_in.shape[2]
    assert w_pool.shape[1:] == (len(POOL_WINDOWS), LANES, LANES)
    assert w_sgu.shape[1:] == (N_SGU_HEADS, CHUNK, CHUNK) and sgu_ln_g.shape[2] == LANES
    assert d_in_proj == 2 * d_pool + 3 * d_sgu and seq % ROW_BLOCK == 0 and ROW_BLOCK % CHUNK == 0
    alpha = (2.0 * depth) ** 0.25

    col_block = d_model
    mod = pl.pallas_call(
        _mod_kernel,
        grid=(depth, 3 * d_model // col_block),
        in_specs=[
            pl.BlockSpec((batch, d_model), lambda l, n: (0, 0)),
            pl.BlockSpec((1, d_model, col_block), lambda l, n: (l, 0, n)),
            pl.BlockSpec((1, 1, col_block), lambda l, n: (l, 0, n)),
        ],
        out_specs=pl.BlockSpec((1, batch, col_block), lambda l, n: (l, 0, n)),
        out_shape=jax.ShapeDtypeStruct((depth, batch, 3 * d_model), _F32),
        name="adaln_mod",
    )(c, w_ada, b_ada.reshape(depth, 1, 3 * d_model))
    mod = mod.reshape(depth, batch, 3, d_model)

    w_in_b = w_in.astype(_BF16)
    w_out_b = w_out.astype(_BF16)
    w_pool_b = w_pool.astype(_BF16)
    causal = jnp.tril(jnp.ones((CHUNK, CHUNK), dtype=bool))
    w_sgu_b = jnp.where(causal[None, None], w_sgu, 0).astype(_BF16)
    sgu_bias = jnp.broadcast_to(b_sgu[:, :, :, None], (depth, N_SGU_HEADS, CHUNK, LANES))
    pool_scale_r = pool_scale.reshape(depth, 1, d_pool)
    sgu_g_r = sgu_ln_g.reshape(depth, 1, d_sgu)
    sgu_b_r = sgu_ln_b.reshape(depth, 1, d_sgu)
    ln_g_r = ln_g.reshape(depth, 1, d_model)
    ln_b_r = ln_b.reshape(depth, 1, d_model)

    body = functools.partial(_layer_kernel, alpha=alpha, d_pool=d_pool, d_sgu=d_sgu)
    for l in range(depth):
        per_layer = functools.partial(_layer_block_index, l)
        x = pl.pallas_call(
            body,
            grid=(batch, seq // ROW_BLOCK),
            in_specs=[
                pl.BlockSpec((1, ROW_BLOCK, d_model), _row_block_index),
                pl.BlockSpec((1, 3, d_model), functools.partial(_mod_block_index, l, batch)),
                pl.BlockSpec((1, d_model, d_in_proj), per_layer(0, 0)),
                pl.BlockSpec((1, len(POOL_WINDOWS), LANES, LANES), per_layer(0, 0, 0)),
                pl.BlockSpec((1, 1, d_pool), per_layer(0, 0)),
                pl.BlockSpec((1, 1, d_sgu), per_layer(0, 0)),
                pl.BlockSpec((1, 1, d_sgu), per_layer(0, 0)),
                pl.BlockSpec((1, N_SGU_HEADS, CHUNK, CHUNK), per_layer(0, 0, 0)),
                pl.BlockSpec((1, N_SGU_HEADS, CHUNK, LANES), per_layer(0, 0, 0)),
                pl.BlockSpec((1, d_pool + d_sgu, d_model), per_layer(0, 0)),
                pl.BlockSpec((1, 1, d_model), per_layer(0, 0)),
                pl.BlockSpec((1, 1, d_model), per_layer(0, 0)),
            ],
            out_specs=pl.BlockSpec((1, ROW_BLOCK, d_model), _row_block_index),
            out_shape=jax.ShapeDtypeStruct((batch, seq, d_model), _F32),
            scratch_shapes=[pltpu.VMEM((HALO + ROW_BLOCK, d_pool), _F32)],
            compiler_params=pltpu.CompilerParams(
                dimension_semantics=("arbitrary", "arbitrary"),
                vmem_limit_bytes=VMEM_LIMIT_BYTES),
            name=f"hybrid_layer_{l}",
        )(x, mod.reshape(depth * batch, 3, d_model), w_in_b, w_pool_b, pool_scale_r, sgu_g_r, sgu_b_r,
          w_sgu_b, sgu_bias, w_out_b, ln_g_r, ln_b_r)
    return x
```

```python
import functools
import math

import jax
import jax.numpy as jnp
from jax import lax
from jax.experimental import pallas as pl
from jax.experimental.pallas import tpu as pltpu

POOL_WINDOWS = (2, 4, 8, 16)
N_SGU_HEADS = 4
CHUNK = 128
LANES = 128
HALO = 16
LN_EPS = 1e-5
ROW_BLOCK = 512
SUB_BLOCK = 256
STRIP = 32
VMEM_LIMIT_BYTES = 48 * 1024 * 1024

_BF16 = jnp.bfloat16
_F32 = jnp.float32


def _layer_norm(x):
    mu = jnp.mean(x, axis=-1, keepdims=True)
    xc = x - mu
    var = jnp.mean(xc * xc, axis=-1, keepdims=True)
    return xc * lax.rsqrt(var + LN_EPS)


def _gelu(x):
    return 0.5 * x * (1.0 + lax.erf(x * (1.0 / math.sqrt(2.0))))


def _layer_block_index(layer, *tail):
    return lambda b, j: (layer,) + tail


def _mod_block_index(layer, batch, b, j):
    return (layer * batch + b, 0, 0)


def _row_block_index(b, j):
    return (b, j, 0)


def _mod_kernel(c_ref, w_ref, b_ref, o_ref):
    sc = jax.nn.silu(c_ref[...]).astype(_BF16)
    o_ref[0] = jnp.dot(sc, w_ref[0].astype(_BF16), preferred_element_type=_F32) + b_ref[0]


def _layer_kernel(x_ref, mod_ref, w_in_ref, w_pool_ref, pool_scale_ref, sgu_g_ref, sgu_b_ref,
                  w_sgu_ref, sgu_bias_ref, w_out_ref, ln_g_ref, ln_b_ref, o_ref, ext_ref, h_ref, ycat_ref,
                  *, alpha, d_pool, d_sgu):
    j = pl.program_id(1)
    rows_per_step = x_ref.shape[1]
    mod = mod_ref[0]
    shift, scale, gate = mod[0:1], mod[1:2], mod[2:3]

    @pl.when(j == 0)
    def _():
        ext_ref[0:HALO, :] = jnp.zeros((HALO, d_pool), _F32)

    subs = [
        _layer_rows(r0, j * rows_per_step + r0, shift, scale, gate,
                    x_ref, w_in_ref, w_pool_ref, pool_scale_ref, sgu_g_ref, sgu_b_ref,
                    w_sgu_ref, sgu_bias_ref, w_out_ref, ln_g_ref, ln_b_ref, o_ref, ext_ref, h_ref, ycat_ref,
                    alpha=alpha, d_pool=d_pool, d_sgu=d_sgu)
        for r0 in range(0, rows_per_step, SUB_BLOCK)]

    advance = lambda k: next(subs[k])
    advance(0)
    advance(0)
    for k in range(len(subs)):
        if k + 1 < len(subs):
            advance(k + 1)
        advance(k)
        if k + 1 < len(subs):
            advance(k + 1)
        advance(k)
        advance(k)

    ext_ref[0:HALO, :] = ext_ref[rows_per_step:rows_per_step + HALO, :]


def _layer_rows(r0, t0, shift, scale, gate,
                x_ref, w_in_ref, w_pool_ref, pool_scale_ref, sgu_g_ref, sgu_b_ref,
                w_sgu_ref, sgu_bias_ref, w_out_ref, ln_g_ref, ln_b_ref, o_ref, ext_ref, h_ref, ycat_ref,
                *, alpha, d_pool, d_sgu):
    sb = SUB_BLOCK
    block = slice(r0, r0 + sb)
    strips = [slice(r0 + s, r0 + s + STRIP) for s in range(0, sb, STRIP)]
    for rows in strips:
        h_ref[rows, :] = (_layer_norm(x_ref[0, rows, :]) * (1.0 + scale) + shift).astype(_BF16)

    def proj(lo, width):
        return jnp.dot(h_ref[block, :], w_in_ref[0, :, lo:lo + width], preferred_element_type=_F32)

    xa = proj(0, d_pool)
    ext_ref[HALO + r0:HALO + r0 + sb, :] = xa
    v = proj(2 * d_pool + d_sgu, d_sgu)
    ga = proj(d_pool, d_pool)
    yield
    u = proj(2 * d_pool, d_sgu)
    gb = proj(2 * d_pool + 2 * d_sgu, d_sgu)
    yield

    t = t0 + lax.broadcasted_iota(jnp.int32, (sb, 1), 0)
    for g, w in enumerate(POOL_WINDOWS):
        lanes = slice(g * LANES, (g + 1) * LANES)
        acc = ext_ref[r0:r0 + HALO + sb, lanes]
        span = 1
        while span < w:
            acc = acc + pltpu.roll(acc, span, axis=0)
            span *= 2
        win_sum = acc[HALO:, :]
        inv_count = 1.0 / jnp.minimum(t + 1, w).astype(_F32)
        pooled = win_sum * inv_count - xa[:, lanes]
        yg = jnp.dot(pooled.astype(_BF16), w_pool_ref[0, g], preferred_element_type=_F32)
        ya = yg * pool_scale_ref[0, :, lanes] * jax.nn.silu(ga[:, lanes])
        ycat_ref[block, lanes] = ya.astype(_BF16)

    n_chunks = sb // CHUNK
    mixed_parts = []
    for h in range(N_SGU_HEADS):
        lanes = slice(h * LANES, (h + 1) * LANES)
        vn = _layer_norm(_gelu(v[:, lanes])) * sgu_g_ref[0, :, lanes] + sgu_b_ref[0, :, lanes]
        vn = vn.astype(_BF16)
        rhs = jnp.concatenate([vn[c * CHUNK:(c + 1) * CHUNK, :] for c in range(n_chunks)], axis=1)
        mixed_parts.append(jnp.dot(w_sgu_ref[0, h], rhs, preferred_element_type=_F32))
    yield

    for h in range(N_SGU_HEADS):
        lanes = slice(h * LANES, (h + 1) * LANES)
        mixed = jnp.concatenate(
            [mixed_parts[h][:, c * LANES:(c + 1) * LANES] + sgu_bias_ref[0, h] for c in range(n_chunks)],
            axis=0)
        yb = _gelu(u[:, lanes]) * mixed * jax.nn.silu(gb[:, lanes])
        ycat_ref[block, d_pool + h * LANES:d_pool + (h + 1) * LANES] = yb.astype(_BF16)
    y = jnp.dot(ycat_ref[block, :], w_out_ref[0], preferred_element_type=_F32)
    yield

    for s, rows in enumerate(strips):
        z = alpha * x_ref[0, rows, :] + gate * y[s * STRIP:(s + 1) * STRIP, :]
        o_ref[0, rows, :] = _layer_norm(z) * ln_g_ref[0] + ln_b_ref[0]
    yield


def kernel(x, c, w_ada, b_ada, w_in, w_pool, pool_scale, sgu_ln_g, sgu_ln_b, w_sgu, b_sgu, w_out, ln_g, ln_b):
    batch, seq, d_model = x.shape
    depth = w_ada.shape[0]
    d_pool = w_pool.shape[1] * w_pool.shape[2]
    d_sgu = w_sgu.shape[1] * sgu_ln_g.shape[2]
    d_in_proj = w_in.shape[2]
    assert w_pool.shape[1:] == (len(POOL_WINDOWS), LANES, LANES)
    assert w_sgu.shape[1:] == (N_SGU_HEADS, CHUNK, CHUNK) and sgu_ln_g.shape[2] == LANES
    assert d_in_proj == 2 * d_pool + 3 * d_sgu
    assert seq % ROW_BLOCK == 0 and ROW_BLOCK % SUB_BLOCK == 0 and SUB_BLOCK % CHUNK == 0
    alpha = (2.0 * depth) ** 0.25

    col_block = d_model
    mod = pl.pallas_call(
        _mod_kernel,
        grid=(depth, 3 * d_model // col_block),
        in_specs=[
            pl.BlockSpec((batch, d_model), lambda l, n: (0, 0)),
            pl.BlockSpec((1, d_model, col_block), lambda l, n: (l, 0, n)),
            pl.BlockSpec((1, 1, col_block), lambda l, n: (l, 0, n)),
        ],
        out_specs=pl.BlockSpec((1, batch, col_block), lambda l, n: (l, 0, n)),
        out_shape=jax.ShapeDtypeStruct((depth, batch, 3 * d_model), _F32),
        name="adaln_mod",
    )(c, w_ada, b_ada.reshape(depth, 1, 3 * d_model))
    mod = mod.reshape(depth * batch, 3, d_model)

    w_in_b = w_in.astype(_BF16)
    w_out_b = w_out.astype(_BF16)
    w_pool_b = w_pool.astype(_BF16)
    causal = jnp.tril(jnp.ones((CHUNK, CHUNK), dtype=bool))
    w_sgu_b = jnp.where(causal[None, None], w_sgu, 0).astype(_BF16)
    sgu_bias = jnp.broadcast_to(b_sgu[:, :, :, None], (depth, N_SGU_HEADS, CHUNK, LANES))
    pool_scale_r = pool_scale.reshape(depth, 1, d_pool)
    sgu_g_r = sgu_ln_g.reshape(depth, 1, d_sgu)
    sgu_b_r = sgu_ln_b.reshape(depth, 1, d_sgu)
    ln_g_r = ln_g.reshape(depth, 1, d_model)
    ln_b_r = ln_b.reshape(depth, 1, d_model)

    body = functools.partial(_layer_kernel, alpha=alpha, d_pool=d_pool, d_sgu=d_sgu)
    for l in range(depth):
        per_layer = functools.partial(_layer_block_index, l)
        x = pl.pallas_call(
            body,
            grid=(batch, seq // ROW_BLOCK),
            in_specs=[
                pl.BlockSpec((1, ROW_BLOCK, d_model), _row_block_index),
                pl.BlockSpec((1, 3, d_model), functools.partial(_mod_block_index, l, batch)),
                pl.BlockSpec((1, d_model, d_in_proj), per_layer(0, 0)),
                pl.BlockSpec((1, len(POOL_WINDOWS), LANES, LANES), per_layer(0, 0, 0)),
                pl.BlockSpec((1, 1, d_pool), per_layer(0, 0)),
                pl.BlockSpec((1, 1, d_sgu), per_layer(0, 0)),
                pl.BlockSpec((1, 1, d_sgu), per_layer(0, 0)),
                pl.BlockSpec((1, N_SGU_HEADS, CHUNK, CHUNK), per_layer(0, 0, 0)),
                pl.BlockSpec((1, N_SGU_HEADS, CHUNK, LANES), per_layer(0, 0, 0)),
                pl.BlockSpec((1, d_pool + d_sgu, d_model), per_layer(0, 0)),
                pl.BlockSpec((1, 1, d_model), per_layer(0, 0)),
                pl.BlockSpec((1, 1, d_model), per_layer(0, 0)),
            ],
            out_specs=pl.BlockSpec((1, ROW_BLOCK, d_model), _row_block_index),
            out_shape=jax.ShapeDtypeStruct((batch, seq, d_model), _F32),
            scratch_shapes=[
                pltpu.VMEM((HALO + ROW_BLOCK, d_pool), _F32),
                pltpu.VMEM((ROW_BLOCK, d_model), _BF16),
                pltpu.VMEM((ROW_BLOCK, d_pool + d_sgu), _BF16),
            ],
            compiler_params=pltpu.CompilerParams(
                dimension_semantics=("arbitrary", "arbitrary"),
                vmem_limit_bytes=VMEM_LIMIT_BYTES),
            name=f"hybrid_layer_{l}",
        )(x, mod, w_in_b, w_pool_b, pool_scale_r, sgu_g_r, sgu_b_r,
          w_sgu_b, sgu_bias, w_out_b, ln_g_r, ln_b_r)
    return x
```
